```python
import math
import jax, jax.numpy as jnp
from jax import lax
import numpy as np

D_MODEL = 1024
BATCH = 8
SEQ = 4096
DEPTH = 4

CHUNK = 64
Q_BLOCK = 128
NORM_EPS = 1e-6
N_MIXERS = 2
N_GDN_LAYERS = (DEPTH + 1) // 2
N_MLA_LAYERS = DEPTH // 2

GDN_K_HEADS = D_MODEL // 128
GDN_V_HEADS = 2 * GDN_K_HEADS
GDN_HEAD_K = 128
GDN_HEAD_V = 128
GDN_CONV = 4
GDN_KEY_DIM = GDN_K_HEADS * GDN_HEAD_K
GDN_VAL_DIM = GDN_V_HEADS * GDN_HEAD_V
GDN_CONV_DIM = 2 * GDN_KEY_DIM + GDN_VAL_DIM
GDN_IN_DIM = GDN_CONV_DIM + GDN_VAL_DIM + 2 * GDN_V_HEADS

MLA_HEADS = D_MODEL // 128
MLA_Q_RANK = 3 * D_MODEL // 8
MLA_KV_RANK = D_MODEL // 4
MLA_NOPE = 128
MLA_ROPE = 64
MLA_V = 128
MLA_IN_DIM = MLA_Q_RANK + MLA_KV_RANK + MLA_ROPE
ROPE_BASE = 10000.0

D_FF = -(-8 * D_MODEL // (3 * 256)) * 256

kernel_name = "hybrid_gdn_mla_chunk_causal_trunk"


def rms_norm(x, gain):
    xf = x.astype(jnp.float32)
    y = xf * lax.rsqrt(jnp.mean(xf * xf, axis=-1, keepdims=True) + NORM_EPS)
    return (y * gain.astype(jnp.float32)).astype(x.dtype)


def l2_norm(x):
    return x * lax.rsqrt(jnp.sum(x * x, axis=-1, keepdims=True) + 1e-6)


def causal_dwconv(x, w):
    k = w.shape[0]
    return lax.conv_general_dilated(
        x, w[:, None, :].astype(x.dtype), window_strides=(1,), padding=[(k - 1, 0)],
        dimension_numbers=("NWC", "WIO", "NWC"), feature_group_count=x.shape[-1])


def gated_delta_rule(q, k, v, g, beta):
    bsz, nh, seq, dk = k.shape
    n_chunks = seq // CHUNK
    q = q * (dk ** -0.5)
    rs = lambda t: t.reshape(bsz, nh, n_chunks, CHUNK, *t.shape[3:])
    q, k, v, g, beta = rs(q), rs(k), rs(v), rs(g), rs(beta)
    g = jnp.cumsum(g, axis=-1)
    tril = jnp.tril(jnp.ones((CHUNK, CHUNK), bool))
    strict = jnp.tril(jnp.ones((CHUNK, CHUNK), bool), k=-1)
    diff = g[..., :, None] - g[..., None, :]
    decay = jnp.where(tril, jnp.exp(jnp.where(tril, diff, 0.0)), 0.0)
    k_beta = k * beta[..., None]
    v_beta = v * beta[..., None]
    a = jnp.where(strict, jnp.einsum("bhncd,bhnjd->bhncj", k_beta, k) * decay, 0.0)
    eye = jnp.eye(CHUNK, dtype=jnp.float32)
    t_inv = lax.linalg.triangular_solve(eye + a, jnp.broadcast_to(eye, a.shape),
                                        left_side=True, lower=True)
    u = jnp.einsum("bhncj,bhnjv->bhncv", t_inv, v_beta)
    w = jnp.einsum("bhncj,bhnjk->bhnck", t_inv, k_beta * jnp.exp(g)[..., None])
    qk = jnp.einsum("bhncd,bhnjd->bhncj", q, k) * decay
    g_last = g[..., -1]
    q_dec = q * jnp.exp(g)[..., None]
    k_dec = k * jnp.exp(g_last[..., None] - g)[..., None]

    def step(state, inp):
        qk_c, u_c, w_c, qd_c, kd_c, gl_c = inp
        v_new = u_c - jnp.einsum("bhck,bhkv->bhcv", w_c, state)
        out = (jnp.einsum("bhck,bhkv->bhcv", qd_c, state)
               + jnp.einsum("bhcj,bhjv->bhcv", qk_c, v_new))
        state = (state * jnp.exp(gl_c)[..., None, None]
                 + jnp.einsum("bhck,bhcv->bhkv", kd_c, v_new))
        return state, out

    mv = lambda t: jnp.moveaxis(t, 2, 0)
    state0 = jnp.zeros((bsz, nh, dk, v.shape[-1]), jnp.float32)
    _, out = lax.scan(step, state0, (mv(qk), mv(u), mv(w), mv(q_dec), mv(k_dec), mv(g_last)))
    return jnp.moveaxis(out, 0, 2).reshape(bsz, nh, seq, v.shape[-1])


def gdn_mixer(h, w_in, conv_w, a_log, dt_bias, out_gain, w_out):
    bsz, seq, _ = h.shape
    proj = h @ w_in
    qkv = proj[..., :GDN_CONV_DIM]
    z = proj[..., GDN_CONV_DIM:GDN_CONV_DIM + GDN_VAL_DIM]
    b = proj[..., GDN_CONV_DIM + GDN_VAL_DIM:GDN_CONV_DIM + GDN_VAL_DIM + GDN_V_HEADS]
    a = proj[..., GDN_CONV_DIM + GDN_VAL_DIM + GDN_V_HEADS:]
    qkv = jax.nn.silu(causal_dwconv(qkv, conv_w))
    q = qkv[..., :GDN_KEY_DIM].reshape(bsz, seq, GDN_K_HEADS, GDN_HEAD_K).astype(jnp.float32)
    k = qkv[..., GDN_KEY_DIM:2 * GDN_KEY_DIM].reshape(bsz, seq, GDN_K_HEADS, GDN_HEAD_K).astype(jnp.float32)
    v = qkv[..., 2 * GDN_KEY_DIM:].reshape(bsz, seq, GDN_V_HEADS, GDN_HEAD_V).astype(jnp.float32)
    rep = GDN_V_HEADS // GDN_K_HEADS
    q = jnp.repeat(l2_norm(q), rep, axis=2)
    k = jnp.repeat(l2_norm(k), rep, axis=2)
    beta = jax.nn.sigmoid(b.astype(jnp.float32))
    g = -jnp.exp(a_log.astype(jnp.float32)) * jax.nn.softplus(
        a.astype(jnp.float32) + dt_bias.astype(jnp.float32))
    tr = lambda t: jnp.swapaxes(t, 1, 2)
    o = gated_delta_rule(tr(q), tr(k), tr(v), tr(g), tr(beta))
    o = jnp.swapaxes(o, 1, 2)
    zf = z.reshape(bsz, seq, GDN_V_HEADS, GDN_HEAD_V).astype(jnp.float32)
    o = (o * lax.rsqrt(jnp.mean(o * o, axis=-1, keepdims=True) + NORM_EPS)
         * out_gain.astype(jnp.float32) * jax.nn.silu(zf))
    return o.reshape(bsz, seq, GDN_VAL_DIM).astype(h.dtype) @ w_out


def rope_cos_sin(positions):
    inv_freq = ROPE_BASE ** (-jnp.arange(0, MLA_ROPE, 2, dtype=jnp.float32) / MLA_ROPE)
    ang = positions.astype(jnp.float32)[..., None] * inv_freq
    return jnp.cos(ang), jnp.sin(ang)


def apply_rope(x, cos, sin):
    xf = x.astype(jnp.float32).reshape(*x.shape[:-1], -1, 2)
    x1, x2 = xf[..., 0], xf[..., 1]
    out = jnp.stack([x1 * cos - x2 * sin, x1 * sin + x2 * cos], axis=-1)
    return out.reshape(x.shape).astype(x.dtype)


def mla_mixer(h, positions, w_in, q_norm_g, w_q_up, kv_norm_g, w_kv_up, w_out):
    bsz, seq, _ = h.shape
    proj = h @ w_in
    c_q = proj[..., :MLA_Q_RANK]
    c_kv = proj[..., MLA_Q_RANK:MLA_Q_RANK + MLA_KV_RANK]
    k_rope = proj[..., MLA_Q_RANK + MLA_KV_RANK:]
    q = (rms_norm(c_q, q_norm_g) @ w_q_up).reshape(bsz, seq, MLA_HEADS, MLA_NOPE + MLA_ROPE)
    kv = (rms_norm(c_kv, kv_norm_g) @ w_kv_up).reshape(bsz, seq, MLA_HEADS, MLA_NOPE + MLA_V)
    q_nope, q_rope = q[..., :MLA_NOPE], q[..., MLA_NOPE:]
    k_nope, v = kv[..., :MLA_NOPE], kv[..., MLA_NOPE:]
    cos, sin = rope_cos_sin(positions)
    q_rope = apply_rope(q_rope, cos[:, :, None, :], sin[:, :, None, :])
    k_rope = apply_rope(k_rope, cos, sin)
    scale = (MLA_NOPE + MLA_ROPE) ** -0.5
    outs = []
    for blk in range(seq // Q_BLOCK):
        q0 = blk * Q_BLOCK
        q1 = q0 + Q_BLOCK
        s = (jnp.einsum("bqhd,bkhd->bhqk", q_nope[:, q0:q1], k_nope[:, :q1])
             + jnp.einsum("bqhr,bkr->bhqk", q_rope[:, q0:q1], k_rope[:, :q1]))
        s = s.astype(jnp.float32) * scale
        q_chunk = jnp.arange(q0, q1) // CHUNK
        k_chunk = jnp.arange(q1) // CHUNK
        mask = k_chunk[None, :] <= q_chunk[:, None]
        p = jax.nn.softmax(jnp.where(mask, s, -jnp.inf), axis=-1).astype(v.dtype)
        outs.append(jnp.einsum("bhqk,bkhd->bqhd", p, v[:, :q1]))
    o = jnp.concatenate(outs, axis=1).reshape(bsz, seq, MLA_HEADS * MLA_V)
    return o @ w_out


def swiglu(h, w_gate_up, w_down):
    gu = h @ w_gate_up
    return (jax.nn.silu(gu[..., :D_FF]) * gu[..., D_FF:]) @ w_down


def setup_inputs(seed: int = 0) -> dict:
    key = jax.random.key(seed)
    ks = jax.random.split(key, 24)
    f32 = jnp.float32

    def dense(k, shape, fan_in):
        return jax.random.normal(k, shape, f32) * fan_in ** -0.5

    def gain(k, shape):
        return 1.0 + 0.02 * jax.random.normal(k, shape, f32)

    x = jax.random.normal(ks[0], (BATCH, SEQ, D_MODEL), f32)
    offsets = jax.random.randint(ks[1], (BATCH, 1), 0, 128) * CHUNK
    positions = (offsets + jnp.arange(SEQ, dtype=jnp.int32)[None, :]).astype(jnp.int32)
    dt = jnp.exp(jax.random.uniform(ks[7], (N_GDN_LAYERS, GDN_V_HEADS), f32,
                                    math.log(1e-3), math.log(1e-1)))
    return {
        "x": x,
        "positions": positions,
        "norm_mix": gain(ks[2], (DEPTH, D_MODEL)),
        "norm_ffn": gain(ks[3], (DEPTH, D_MODEL)),
        "gdn_w_in": dense(ks[4], (N_GDN_LAYERS, D_MODEL, GDN_IN_DIM), D_MODEL),
        "gdn_conv_w": dense(ks[5], (N_GDN_LAYERS, GDN_CONV, GDN_CONV_DIM), GDN_CONV),
        "gdn_a_log": jnp.log(jax.random.uniform(ks[6], (N_GDN_LAYERS, GDN_V_HEADS), f32, 1.0, 16.0)),
        "gdn_dt_bias": dt + jnp.log(-jnp.expm1(-dt)),
        "gdn_out_norm": gain(ks[8], (N_GDN_LAYERS, GDN_HEAD_V)),
        "gdn_w_out": dense(ks[9], (N_GDN_LAYERS, GDN_VAL_DIM, D_MODEL), GDN_VAL_DIM),
        "mla_w_in": dense(ks[10], (N_MLA_LAYERS, D_MODEL, MLA_IN_DIM), D_MODEL),
        "mla_q_norm": gain(ks[11], (N_MLA_LAYERS, MLA_Q_RANK)),
        "mla_w_q_up": dense(ks[12], (N_MLA_LAYERS, MLA_Q_RANK, MLA_HEADS * (MLA_NOPE + MLA_ROPE)), MLA_Q_RANK),
        "mla_kv_norm": gain(ks[13], (N_MLA_LAYERS, MLA_KV_RANK)),
        "mla_w_kv_up": dense(ks[14], (N_MLA_LAYERS, MLA_KV_RANK, MLA_HEADS * (MLA_NOPE + MLA_V)), MLA_KV_RANK),
        "mla_w_out": dense(ks[15], (N_MLA_LAYERS, MLA_HEADS * MLA_V, D_MODEL), MLA_HEADS * MLA_V),
        "ffn_w_gate_up": dense(ks[16], (DEPTH, D_MODEL, 2 * D_FF), D_MODEL),
        "ffn_w_down": dense(ks[17], (DEPTH, D_FF, D_MODEL), D_FF),
        "final_norm": gain(ks[18], (D_MODEL,)),
    }


def reference(x, positions, norm_mix, norm_ffn, gdn_w_in, gdn_conv_w, gdn_a_log, gdn_dt_bias,
              gdn_out_norm, gdn_w_out, mla_w_in, mla_q_norm, mla_w_q_up, mla_kv_norm,
              mla_w_kv_up, mla_w_out, ffn_w_gate_up, ffn_w_down, final_norm):
    for i in range(DEPTH):
        h = rms_norm(x, norm_mix[i])
        j = i // N_MIXERS
        if i % N_MIXERS == 0:
            mix = gdn_mixer(h, gdn_w_in[j], gdn_conv_w[j], gdn_a_log[j], gdn_dt_bias[j],
                            gdn_out_norm[j], gdn_w_out[j])
        else:
            mix = mla_mixer(h, positions, mla_w_in[j], mla_q_norm[j], mla_w_q_up[j],
                            mla_kv_norm[j], mla_w_kv_up[j], mla_w_out[j])
        x = x + mix.astype(x.dtype)
        x = x + swiglu(rms_norm(x, norm_ffn[i]), ffn_w_gate_up[i], ffn_w_down[i]).astype(x.dtype)
    return rms_norm(x, final_norm)
```

```python
import functools
import math

import jax
import jax.numpy as jnp
from jax import lax
from jax.experimental import pallas as pl
from jax.experimental.pallas import tpu as pltpu

F32 = jnp.float32
BF16 = jnp.bfloat16

NORM_EPS = 1e-6
L2_EPS = 1e-6
CHUNK = 64
ROPE_BASE = 10000.0
LANES = 128
SUBLANES = 8
VMEM_LIMIT = 56 * 1024 * 1024

HEAD = 128
MLA_ROPE = 64
MLA_QK = 2 * HEAD

TM = 512
GDN_TC = 256
ATT_T = 512


def _resident(shape):
    nd = len(shape)
    return pl.BlockSpec(shape, lambda *_: (0,) * nd, pipeline_mode=pl.Buffered(1))


def _params(sem):
    return pltpu.CompilerParams(dimension_semantics=sem, vmem_limit_bytes=VMEM_LIMIT)


def _rms(x, gain):
    return x * lax.rsqrt(jnp.mean(x * x, axis=-1, keepdims=True) + NORM_EPS) * gain


def _silu(x):
    return x * jax.nn.sigmoid(x)


def _mm(a, b):
    return jnp.dot(a, b, preferred_element_type=F32)


def _mm_nt(a, b):
    return lax.dot_general(a, b, (((1,), (1,)), ((), ())), preferred_element_type=F32)


def _mm_tn(a, b):
    return lax.dot_general(a, b, (((0,), (0,)), ((), ())), preferred_element_type=F32)


def _ffn_kernel(x_ref, g_ref, wg_ref, wu_ref, wd_ref, fg_ref, o_ref, *, n_chunks, final_norm):
    x = x_ref[...]
    h = _rms(x, g_ref[...]).astype(BF16)
    acc = x
    for c in range(n_chunks):
        gate = _mm(h, wg_ref[c])
        up = _mm(h, wu_ref[c])
        act = (_silu(gate) * up).astype(BF16)
        acc = acc + _mm(act, wd_ref[c])
    if final_norm:
        acc = _rms(acc, fg_ref[...])
    o_ref[...] = acc


def _ffn(x, gain, wg, wu, wd, final_gain, final_norm):
    t, d = x.shape
    n_chunks, _, fc = wg.shape
    kern = functools.partial(_ffn_kernel, n_chunks=n_chunks, final_norm=final_norm)
    return pl.pallas_call(
        kern,
        grid=(t // TM,),
        in_specs=[
            pl.BlockSpec((TM, d), lambda i: (i, 0)),
            _resident((1, d)),
            _resident(wg.shape),
            _resident(wu.shape),
            _resident(wd.shape),
            _resident((1, d)),
        ],
        out_specs=pl.BlockSpec((TM, d), lambda i: (i, 0)),
        out_shape=jax.ShapeDtypeStruct((t, d), F32),
        compiler_params=_params(("parallel",)),
        name="ffn",
    )(x, gain, wg, wu, wd, final_gain)


def _gdn_in_kernel(x_ref, g_ref, wqkv_ref, wz_ref, wba_ref, cw_ref,
                   q_ref, k_ref, v_ref, z_ref, ba_ref, buf_ref, *, key_dim, conv_k):
    tm = x_ref.shape[0]
    c_dim = wqkv_ref.shape[1]
    halo = SUBLANES

    @pl.when(pl.program_id(1) == 0)
    def _():
        buf_ref[0:halo, :] = jnp.zeros((halo, c_dim), F32)

    h = _rms(x_ref[...], g_ref[...]).astype(BF16)
    z_ref[...] = _mm(h, wz_ref[...]).astype(BF16)
    ba_ref[...] = _mm(h, wba_ref[...])
    col_blk = 4 * LANES
    for c0 in range(0, c_dim, col_blk):
        buf_ref[halo:halo + tm, c0:c0 + col_blk] = _mm(h, wqkv_ref[:, c0:c0 + col_blk])

    q_scale = HEAD ** -0.5
    for c0 in range(0, c_dim, LANES):
        cs = slice(c0, c0 + LANES)
        y = None
        for j in range(conv_k):
            tap = buf_ref[pl.ds(halo - (conv_k - 1) + j, tm), cs] * cw_ref[j:j + 1, cs]
            y = tap if y is None else y + tap
        y = _silu(y)
        if c0 < 2 * key_dim:
            y = y * lax.rsqrt(jnp.sum(y * y, axis=-1, keepdims=True) + L2_EPS)
            if c0 < key_dim:
                q_ref[:, cs] = (y * q_scale).astype(BF16)
            else:
                k_ref[:, c0 - key_dim:c0 - key_dim + LANES] = y.astype(BF16)
        else:
            v_ref[:, c0 - 2 * key_dim:c0 - 2 * key_dim + LANES] = y.astype(BF16)

    buf_ref[0:halo, :] = buf_ref[tm:tm + halo, :]


def _gdn_in(x, gain, wqkv, wz, wba, conv_w, bsz, seq, key_dim, val_dim):
    t, d = x.shape
    ns = seq // TM
    c_dim = wqkv.shape[1]
    row = lambda b, s: (b * ns + s, 0)
    kern = functools.partial(_gdn_in_kernel, key_dim=key_dim, conv_k=conv_w.shape[0])
    return pl.pallas_call(
        kern,
        grid=(bsz, ns),
        in_specs=[
            pl.BlockSpec((TM, d), row),
            _resident((1, d)),
            _resident(wqkv.shape),
            _resident(wz.shape),
            _resident(wba.shape),
            _resident(conv_w.shape),
        ],
        out_specs=[
            pl.BlockSpec((TM, key_dim), row),
            pl.BlockSpec((TM, key_dim), row),
            pl.BlockSpec((TM, val_dim), row),
            pl.BlockSpec((TM, val_dim), row),
            pl.BlockSpec((TM, LANES), row),
        ],
        out_shape=[
            jax.ShapeDtypeStruct((t, key_dim), BF16),
            jax.ShapeDtypeStruct((t, key_dim), BF16),
            jax.ShapeDtypeStruct((t, val_dim), BF16),
            jax.ShapeDtypeStruct((t, val_dim), BF16),
            jax.ShapeDtypeStruct((t, LANES), F32),
        ],
        scratch_shapes=[pltpu.VMEM((TM + 2 * SUBLANES, c_dim), F32)],
        compiler_params=_params(("arbitrary", "arbitrary")),
        name="gdn_in",
    )(x, gain, wqkv, wz, wba, conv_w)


def _gdn_delta_kernel(x_ref, q_ref, k_ref, v_ref, z_ref, ba_ref, alog_ref, dtb_ref, og_ref, wo_ref,
                      o_ref, state_ref, out_ref, *, n_vh, rep):
    tc = x_ref.shape[0]
    n_chunks = tc // CHUNK

    @pl.when(pl.program_id(1) == 0)
    def _():
        state_ref[...] = jnp.zeros(state_ref.shape, F32)

    ba = ba_ref[...]
    beta_all = jax.nn.sigmoid(ba)
    pre = ba + dtb_ref[...]
    softplus = jnp.maximum(pre, 0.0) + jnp.log1p(jnp.exp(-jnp.abs(pre)))
    g_all = -jnp.exp(alog_ref[...]) * softplus
    row_in_chunk = lax.broadcasted_iota(jnp.int32, (tc, LANES), 0) % CHUNK
    shift = 1
    while shift < CHUNK:
        g_all = g_all + jnp.where(row_in_chunk >= shift, pltpu.roll(g_all, shift, axis=0), 0.0)
        shift *= 2
    g_t = g_all.T

    ri = lax.broadcasted_iota(jnp.int32, (CHUNK, CHUNK), 0)
    ci = lax.broadcasted_iota(jnp.int32, (CHUNK, CHUNK), 1)
    tril = ri >= ci
    strict = ri > ci

    for c in range(n_chunks):
        r0 = c * CHUNK
        rows = slice(r0, r0 + CHUNK)
        for hk in range(n_vh // rep):
            ks = slice(hk * HEAD, (hk + 1) * HEAD)
            q_c = q_ref[rows, ks]
            k_c = k_ref[rows, ks]
            qk = _mm_nt(q_c, k_c)
            kk = _mm_nt(k_c, k_c)
            k_f = k_c.astype(F32)
            q_f = q_c.astype(F32)
            for hv in range(hk * rep, (hk + 1) * rep):
                vs = slice(hv * HEAD, (hv + 1) * HEAD)
                g_col = g_all[rows, n_vh + hv:n_vh + hv + 1]
                g_row = g_t[n_vh + hv:n_vh + hv + 1, rows]
                beta = beta_all[rows, hv:hv + 1]
                g_last = g_col[CHUNK - 1:CHUNK, :]
                decay = jnp.where(tril, jnp.exp(jnp.where(tril, g_col - g_row, 0.0)), 0.0)
                a = jnp.where(strict, beta * kk * decay, 0.0)
                eg = jnp.exp(g_col)
                rhs = jnp.concatenate(
                    [v_ref[rows, vs].astype(F32) * beta, k_f * (beta * eg)], axis=1)
                a_b = a.astype(BF16)
                xs = rhs - _mm(a_b, rhs.astype(BF16))
                p = a_b
                span = 2
                while span < CHUNK:
                    p = _mm(p, p).astype(BF16)
                    xs = xs + _mm(p, xs.astype(BF16))
                    span *= 2
                u = xs[:, :HEAD]
                w = xs[:, HEAD:]
                s_old = state_ref[hv]
                s_b = s_old.astype(BF16)
                v_new = u - _mm(w.astype(BF16), s_b)
                v_new_b = v_new.astype(BF16)
                out = _mm((q_f * eg).astype(BF16), s_b) + _mm((qk * decay).astype(BF16), v_new_b)
                k_dec = (k_f * jnp.exp(g_last - g_col)).astype(BF16)
                state_ref[hv] = s_old * jnp.exp(g_last) + _mm_tn(k_dec, v_new_b)
                out_ref[rows, vs] = out

    for hv in range(n_vh):
        vs = slice(hv * HEAD, (hv + 1) * HEAD)
        o = out_ref[:, vs]
        o = _rms(o, og_ref[...]) * _silu(z_ref[:, vs].astype(F32))
        out_ref[:, vs] = o
    o_ref[...] = x_ref[...] + _mm(out_ref[...].astype(BF16), wo_ref[...])


def _gdn_delta(x, q, k, v, z, ba, alog, dtb, out_gain, wo, bsz, seq, n_vh, rep):
    t, d = x.shape
    ns = seq // GDN_TC
    key_dim = q.shape[1]
    val_dim = v.shape[1]
    row = lambda b, s: (b * ns + s, 0)
    kern = functools.partial(_gdn_delta_kernel, n_vh=n_vh, rep=rep)
    return pl.pallas_call(
        kern,
        grid=(bsz, ns),
        in_specs=[
            pl.BlockSpec((GDN_TC, d), row),
            pl.BlockSpec((GDN_TC, key_dim), row),
            pl.BlockSpec((GDN_TC, key_dim), row),
            pl.BlockSpec((GDN_TC, val_dim), row),
            pl.BlockSpec((GDN_TC, val_dim), row),
            pl.BlockSpec((GDN_TC, LANES), row),
            _resident((1, LANES)),
            _resident((1, LANES)),
            _resident((1, HEAD)),
            _resident(wo.shape),
        ],
        out_specs=pl.BlockSpec((GDN_TC, d), row),
        out_shape=jax.ShapeDtypeStruct((t, d), F32),
        scratch_shapes=[
            pltpu.VMEM((n_vh, HEAD, HEAD), F32),
            pltpu.VMEM((GDN_TC, val_dim), F32),
        ],
        compiler_params=_params(("arbitrary", "arbitrary")),
        name="gdn_delta",
    )(x, q, k, v, z, ba, alog, dtb, out_gain, wo)


def _mla_in_kernel(x_ref, pos_ref, g_ref, win_ref, qg_ref, wq_ref, kvg_ref, wkv_ref, freq_ref, sign_ref,
                   q_ref, k_ref, v_ref, *, q_rank, kv_rank, n_heads, scale):
    tm = x_ref.shape[0]
    h = _rms(x_ref[...], g_ref[...]).astype(BF16)
    proj = _mm(h, win_ref[...])
    c_q = _rms(proj[:, :q_rank], qg_ref[...]).astype(BF16)
    c_kv = _rms(proj[:, q_rank:q_rank + kv_rank], kvg_ref[...]).astype(BF16)
    k_rope = proj[:, q_rank + kv_rank:]

    ang = pos_ref[...].astype(F32) * freq_ref[...]
    lane = lax.broadcasted_iota(jnp.int32, (tm, LANES), 1)
    lower = lane < MLA_ROPE
    table = jnp.where(lower, jnp.cos(ang), jnp.sin(ang) * sign_ref[...])

    def rotate(blk):
        t = blk * table
        return jnp.where(lower, t + pltpu.roll(t, MLA_ROPE, axis=1), 0.0)

    k_rot = rotate(k_rope).astype(BF16)
    q_all = _mm(c_q, wq_ref[...])
    kv_all = _mm(c_kv, wkv_ref[...])
    for hd in range(n_heads):
        qb = q_all[:, hd * MLA_QK:(hd + 1) * MLA_QK]
        q_ref[0, hd, :, 0:HEAD] = (qb[:, :HEAD] * scale).astype(BF16)
        q_ref[0, hd, :, HEAD:MLA_QK] = (rotate(qb[:, HEAD:]) * scale).astype(BF16)
        kvb = kv_all[:, hd * 2 * HEAD:(hd + 1) * 2 * HEAD]
        k_ref[0, hd, :, 0:HEAD] = kvb[:, :HEAD].astype(BF16)
        k_ref[0, hd, :, HEAD:MLA_QK] = k_rot
        v_ref[0, hd, :, :] = kvb[:, HEAD:].astype(BF16)


def _mla_in(x, pos, gain, win, qg, wq, kvg, wkv, freq, sign, bsz, seq, n_heads, q_rank, kv_rank, scale):
    t, d = x.shape
    ns = seq // TM
    row = lambda b, s: (b * ns + s, 0)
    hrow = lambda b, s: (b, 0, s, 0)
    kern = functools.partial(_mla_in_kernel, q_rank=q_rank, kv_rank=kv_rank, n_heads=n_heads, scale=scale)
    return pl.pallas_call(
        kern,
        grid=(bsz, ns),
        in_specs=[
            pl.BlockSpec((TM, d), row),
            pl.BlockSpec((TM, 1), row),
            _resident((1, d)),
            _resident(win.shape),
            _resident((1, q_rank)),
            _resident(wq.shape),
            _resident((1, kv_rank)),
            _resident(wkv.shape),
            _resident((1, LANES)),
            _resident((1, LANES)),
        ],
        out_specs=[
            pl.BlockSpec((1, n_heads, TM, MLA_QK), hrow),
            pl.BlockSpec((1, n_heads, TM, MLA_QK), hrow),
            pl.BlockSpec((1, n_heads, TM, HEAD), hrow),
        ],
        out_shape=[
            jax.ShapeDtypeStruct((bsz, n_heads, seq, MLA_QK), BF16),
            jax.ShapeDtypeStruct((bsz, n_heads, seq, MLA_QK), BF16),
            jax.ShapeDtypeStruct((bsz, n_heads, seq, HEAD), BF16),
        ],
        compiler_params=_params(("parallel", "parallel")),
        name="mla_in",
    )(x, pos, gain, win, qg, wq, kvg, wkv, freq, sign)


def _mla_attn_kernel(q_ref, k_ref, v_ref, o_ref):
    tq = q_ref.shape[2]
    i = pl.program_id(2)
    q = q_ref[0, 0]

    def step(j, carry, masked):
        m, l, acc = carry
        kv_rows = pl.ds(pl.multiple_of(j * tq, tq), tq)
        s = _mm_nt(q, k_ref[0, 0, kv_rows, :])
        if masked:
            qc = lax.broadcasted_iota(jnp.int32, (tq, tq), 0) // CHUNK
            kc = lax.broadcasted_iota(jnp.int32, (tq, tq), 1) // CHUNK
            s = jnp.where(kc <= qc, s, -jnp.inf)
        m_new = jnp.maximum(m, jnp.max(s, axis=-1, keepdims=True))
        alpha = jnp.exp(m - m_new)
        p = jnp.exp(s - m_new)
        l = alpha * l + jnp.sum(p, axis=-1, keepdims=True)
        acc = alpha * acc + _mm(p.astype(BF16), v_ref[0, 0, kv_rows, :])
        return m_new, l, acc

    init = (jnp.full((tq, 1), -jnp.inf, F32), jnp.zeros((tq, 1), F32), jnp.zeros((tq, HEAD), F32))
    carry = lax.fori_loop(0, i, lambda j, c: step(j, c, False), init)
    _, l, acc = step(i, carry, True)
    o_ref[0] = (acc / l).astype(BF16)


def _mla_attn(q, k, v):
    bsz, n_heads, seq, _ = q.shape
    nq = seq // ATT_T
    return pl.pallas_call(
        _mla_attn_kernel,
        grid=(bsz, n_heads, nq),
        in_specs=[
            pl.BlockSpec((1, 1, ATT_T, MLA_QK), lambda b, h, i: (b, h, i, 0)),
            pl.BlockSpec((1, 1, seq, MLA_QK), lambda b, h, i: (b, h, 0, 0)),
            pl.BlockSpec((1, 1, seq, HEAD), lambda b, h, i: (b, h, 0, 0)),
        ],
        out_specs=pl.BlockSpec((1, ATT_T, HEAD), lambda b, h, i: (b, i, h)),
        out_shape=jax.ShapeDtypeStruct((bsz, seq, n_heads * HEAD), BF16),
        compiler_params=_params(("parallel", "parallel", "arbitrary")),
        name="mla_attn",
    )(q, k, v)


def _proj_residual_kernel(x_ref, a_ref, w_ref, o_ref):
    o_ref[...] = x_ref[...] + _mm(a_ref[...], w_ref[...])


def _proj_residual(x, a, w):
    t, d = x.shape
    return pl.pallas_call(
        _proj_residual_kernel,
        grid=(t // TM,),
        in_specs=[
            pl.BlockSpec((TM, d), lambda i: (i, 0)),
            pl.BlockSpec((TM, a.shape[1]), lambda i: (i, 0)),
            _resident(w.shape),
        ],
        out_specs=pl.BlockSpec((TM, d), lambda i: (i, 0)),
        out_shape=jax.ShapeDtypeStruct((t, d), F32),
        compiler_params=_params(("parallel",)),
        name="mla_out",
    )(x, a, w)


def _pad_lanes(a):
    return jnp.zeros((1, LANES), F32).at[0, :a.shape[0]].set(a.astype(F32))


def _pair_swap(w):
    n = w.shape[-1]
    return w.reshape(*w.shape[:-1], n // 2, 2)[..., ::-1].reshape(w.shape)


def kernel(x, positions, norm_mix, norm_ffn, gdn_w_in, gdn_conv_w, gdn_a_log, gdn_dt_bias, gdn_out_norm, gdn_w_out, mla_w_in, mla_q_norm, mla_w_q_up, mla_kv_norm, mla_w_kv_up, mla_w_out, ffn_w_gate_up, ffn_w_down, final_norm):
    bsz, seq, d = x.shape
    depth = norm_mix.shape[0]
    t = bsz * seq
    d_ff = ffn_w_down.shape[1]
    ff_chunk = 2 * LANES
    n_ff = d_ff // ff_chunk

    n_vh = gdn_a_log.shape[1]
    val_dim = gdn_w_out.shape[1]
    conv_dim = gdn_conv_w.shape[2]
    key_dim = (conv_dim - val_dim) // 2
    rep = n_vh // (key_dim // HEAD)

    q_rank = mla_q_norm.shape[1]
    kv_rank = mla_kv_norm.shape[1]
    n_heads = mla_w_out.shape[1] // HEAD
    scale = (HEAD + MLA_ROPE) ** -0.5

    inv_freq = ROPE_BASE ** (-jnp.arange(0, MLA_ROPE, 2, dtype=F32) / MLA_ROPE)
    freq_row = jnp.tile(jnp.repeat(inv_freq, 2), 2)[None, :]
    sign_row = jnp.tile(jnp.tile(jnp.array([-1.0, 1.0], F32), MLA_ROPE // 2), 2)[None, :]
    pos_col = positions.reshape(t, 1)

    xf = x.reshape(t, d)
    for i in range(depth):
        j = i // 2
        gain = norm_mix[i][None, :]
        if i % 2 == 0:
            w_in = gdn_w_in[j]
            wqkv = w_in[:, :conv_dim].astype(BF16)
            wz = w_in[:, conv_dim:conv_dim + val_dim].astype(BF16)
            wba = jnp.zeros((d, LANES), F32).at[:, :2 * n_vh].set(w_in[:, conv_dim + val_dim:]).astype(BF16)
            q, k, v, z, ba = _gdn_in(xf, gain, wqkv, wz, wba, gdn_conv_w[j], bsz, seq, key_dim, val_dim)
            alog = jnp.zeros((1, LANES), F32).at[0, n_vh:2 * n_vh].set(gdn_a_log[j])
            dtb = jnp.zeros((1, LANES), F32).at[0, n_vh:2 * n_vh].set(gdn_dt_bias[j])
            xf = _gdn_delta(xf, q, k, v, z, ba, alog, dtb, gdn_out_norm[j][None, :],
                            gdn_w_out[j].astype(BF16), bsz, seq, n_vh, rep)
        else:
            w_in = mla_w_in[j]
            w_kr = w_in[:, q_rank + kv_rank:]
            win = jnp.concatenate([w_in, _pair_swap(w_kr)], axis=1).astype(BF16)
            wq = mla_w_q_up[j].reshape(q_rank, n_heads, HEAD + MLA_ROPE)
            wq = jnp.concatenate([wq, _pair_swap(wq[..., HEAD:])], axis=-1)
            wq = wq.reshape(q_rank, n_heads * MLA_QK).astype(BF16)
            qh, kh, vh = _mla_in(xf, pos_col, gain, win, mla_q_norm[j][None, :], wq,
                                 mla_kv_norm[j][None, :], mla_w_kv_up[j].astype(BF16),
                                 freq_row, sign_row, bsz, seq, n_heads, q_rank, kv_rank, scale)
            att = _mla_attn(qh, kh, vh).reshape(t, n_heads * HEAD)
            xf = _proj_residual(xf, att, mla_w_out[j].astype(BF16))
        wgu = ffn_w_gate_up[i]
        wg = wgu[:, :d_ff].reshape(d, n_ff, ff_chunk).transpose(1, 0, 2).astype(BF16)
        wu = wgu[:, d_ff:].reshape(d, n_ff, ff_chunk).transpose(1, 0, 2).astype(BF16)
        wd = ffn_w_down[i].reshape(n_ff, ff_chunk, d).astype(BF16)
        last = i == depth - 1
        xf = _ffn(xf, norm_ffn[i][None, :], wg, wu, wd, final_norm[None, :], last)
    return xf.reshape(bsz, seq, d)
```

```python
import functools

import jax
import jax.numpy as jnp
from jax import lax
from jax.experimental import pallas as pl
from jax.experimental.pallas import tpu as pltpu

F32 = jnp.float32
BF16 = jnp.bfloat16

NORM_EPS = 1e-6
L2_EPS = 1e-6
CHUNK = 64
ROPE_BASE = 10000.0
LANES = 128
SUBLANES = 8
VMEM_LIMIT = 56 * 1024 * 1024

HEAD = 128
MLA_ROPE = 64
MLA_QK = 2 * HEAD

TM = 512
GDN_NC = 4
GDN_TC = GDN_NC * CHUNK
ATT_T = 512


def _resident(shape):
    nd = len(shape)
    return pl.BlockSpec(shape, lambda *_: (0,) * nd, pipeline_mode=pl.Buffered(1))


def _params(sem):
    return pltpu.CompilerParams(dimension_semantics=sem, vmem_limit_bytes=VMEM_LIMIT)


def _rms(x, gain):
    return x * lax.rsqrt(jnp.mean(x * x, axis=-1, keepdims=True) + NORM_EPS) * gain


def _silu(x):
    return x * jax.nn.sigmoid(x)


def _mm(a, b):
    return jnp.dot(a, b, preferred_element_type=F32)


def _mm_nt(a, b):
    return lax.dot_general(a, b, (((1,), (1,)), ((), ())), preferred_element_type=F32)


def _ffn_kernel(x_ref, g_ref, wg_ref, wu_ref, wd_ref, fg_ref, o_ref, *, n_chunks, final_norm):
    x = x_ref[...]
    h = _rms(x, g_ref[...]).astype(BF16)
    acc = x
    for c in range(n_chunks):
        gate = _mm(h, wg_ref[c])
        up = _mm(h, wu_ref[c])
        act = (_silu(gate) * up).astype(BF16)
        acc = acc + _mm(act, wd_ref[c])
    if final_norm:
        acc = _rms(acc, fg_ref[...])
    o_ref[...] = acc


def _ffn(x, gain, wg, wu, wd, final_gain, final_norm):
    t, d = x.shape
    n_chunks = wg.shape[0]
    kern = functools.partial(_ffn_kernel, n_chunks=n_chunks, final_norm=final_norm)
    return pl.pallas_call(
        kern,
        grid=(t // TM,),
        in_specs=[
            pl.BlockSpec((TM, d), lambda i: (i, 0)),
            _resident((1, d)),
            _resident(wg.shape),
            _resident(wu.shape),
            _resident(wd.shape),
            _resident((1, d)),
        ],
        out_specs=pl.BlockSpec((TM, d), lambda i: (i, 0)),
        out_shape=jax.ShapeDtypeStruct((t, d), F32),
        compiler_params=_params(("parallel",)),
        name="ffn",
    )(x, gain, wg, wu, wd, final_gain)


def _gdn_in_kernel(x_ref, g_ref, wqkv_ref, wz_ref, wba_ref, cw_ref,
                   q_ref, k_ref, v_ref, z_ref, b_ref, a_ref, buf_ref, *, key_dim, conv_k):
    tm = x_ref.shape[0]
    c_dim = wqkv_ref.shape[1]
    halo = SUBLANES

    @pl.when(pl.program_id(1) == 0)
    def _():
        buf_ref[0:halo, :] = jnp.zeros((halo, c_dim), F32)

    h = _rms(x_ref[...], g_ref[...]).astype(BF16)
    z_ref[...] = _mm(h, wz_ref[...]).astype(BF16)
    ba = _mm(h, wba_ref[...])
    b_ref[...] = ba[:, :LANES]
    a_ref[...] = ba[:, LANES:]
    col_blk = 4 * LANES
    for c0 in range(0, c_dim, col_blk):
        buf_ref[halo:halo + tm, c0:c0 + col_blk] = _mm(h, wqkv_ref[:, c0:c0 + col_blk])

    q_scale = HEAD ** -0.5
    for c0 in range(0, c_dim, LANES):
        cs = slice(c0, c0 + LANES)
        ext = buf_ref[0:halo + tm, cs]
        y = None
        for j in range(conv_k):
            back = conv_k - 1 - j
            frames = pltpu.roll(ext, back, axis=0)[halo:] if back else ext[halo:]
            tap = frames * cw_ref[j:j + 1, cs]
            y = tap if y is None else y + tap
        y = _silu(y)
        if c0 < 2 * key_dim:
            y = y * lax.rsqrt(jnp.sum(y * y, axis=-1, keepdims=True) + L2_EPS)
            if c0 < key_dim:
                q_ref[:, cs] = (y * q_scale).astype(BF16)
            else:
                k_ref[:, c0 - key_dim:c0 - key_dim + LANES] = y.astype(BF16)
        else:
            v_ref[:, c0 - 2 * key_dim:c0 - 2 * key_dim + LANES] = y.astype(BF16)

    buf_ref[0:halo, :] = buf_ref[tm:tm + halo, :]


def _gdn_in(x, gain, wqkv, wz, wba, conv_w, bsz, seq, key_dim, val_dim):
    t, d = x.shape
    ns = seq // TM
    c_dim = wqkv.shape[1]
    row = lambda b, s: (b * ns + s, 0)
    kern = functools.partial(_gdn_in_kernel, key_dim=key_dim, conv_k=conv_w.shape[0])
    return pl.pallas_call(
        kern,
        grid=(bsz, ns),
        in_specs=[
            pl.BlockSpec((TM, d), row),
            _resident((1, d)),
            _resident(wqkv.shape),
            _resident(wz.shape),
            _resident(wba.shape),
            _resident(conv_w.shape),
        ],
        out_specs=[
            pl.BlockSpec((TM, key_dim), row),
            pl.BlockSpec((TM, key_dim), row),
            pl.BlockSpec((TM, val_dim), row),
            pl.BlockSpec((TM, val_dim), row),
            pl.BlockSpec((TM, LANES), row),
            pl.BlockSpec((TM, LANES), row),
        ],
        out_shape=[
            jax.ShapeDtypeStruct((t, key_dim), BF16),
            jax.ShapeDtypeStruct((t, key_dim), BF16),
            jax.ShapeDtypeStruct((t, val_dim), BF16),
            jax.ShapeDtypeStruct((t, val_dim), BF16),
            jax.ShapeDtypeStruct((t, LANES), F32),
            jax.ShapeDtypeStruct((t, LANES), F32),
        ],
        scratch_shapes=[pltpu.VMEM((TM + 2 * SUBLANES, c_dim), F32)],
        compiler_params=_params(("arbitrary", "arbitrary")),
        name="gdn_in",
    )(x, gain, wqkv, wz, wba, conv_w)


def _gdn_delta_kernel(x_ref, q_ref, k_ref, v_ref, z_ref, b_ref, a_ref, alog_ref, dtb_ref, og_ref, wo_ref,
                      o_ref, state_ref, out_ref, *, n_vh, rep):
    tc = x_ref.shape[0]
    nc = tc // CHUNK
    n_kh = n_vh // rep
    w_sbs = nc * CHUNK

    @pl.when(pl.program_id(1) == 0)
    def _():
        state_ref[...] = jnp.zeros(state_ref.shape, F32)

    beta_c = jax.nn.sigmoid(b_ref[...])
    pre = a_ref[...] + dtb_ref[...]
    softplus = jnp.maximum(pre, 0.0) + jnp.log1p(jnp.exp(-jnp.abs(pre)))
    g_c = -jnp.exp(alog_ref[...]) * softplus
    row_in_chunk = lax.broadcasted_iota(jnp.int32, (tc, LANES), 0) % CHUNK
    shift = 1
    while shift < CHUNK:
        g_c = g_c + jnp.where(row_in_chunk >= shift, pltpu.roll(g_c, shift, axis=0), 0.0)
        shift *= 2
    glast_c = jnp.concatenate(
        [jnp.broadcast_to(g_c[(c + 1) * CHUNK - 1:(c + 1) * CHUNK, :], (CHUNK, LANES)) for c in range(nc)], axis=0)
    g_t = g_c.T[:n_vh]
    beta_t = beta_c.T[:n_vh]
    glast_t = glast_c.T[:n_vh]
    eg_t = jnp.exp(g_t)
    dkb_t = jnp.exp(glast_t - g_t) * beta_t
    egl_t = jnp.exp(glast_t)
    lane_t = lax.broadcasted_iota(jnp.int32, (n_vh, LANES), 1)
    egl_rows = []
    for c in range(nc):
        blk = egl_t[:, (c // 2) * LANES:(c // 2 + 1) * LANES]
        swapped = pltpu.roll(blk, CHUNK, axis=1)
        egl_rows.append(jnp.where((lane_t < CHUNK) == (c % 2 == 0), blk, swapped))

    ri = lax.broadcasted_iota(jnp.int32, (CHUNK, w_sbs), 0)
    li = lax.broadcasted_iota(jnp.int32, (CHUNK, w_sbs), 1)
    lj = li % CHUNK
    lblk = li // CHUNK
    tril = ri >= lj
    strict = ri > lj
    eye = ri == lj
    lane_low = lax.broadcasted_iota(jnp.int32, (CHUNK, LANES), 1) < CHUNK
    bd_mask = jnp.where(lax.broadcasted_iota(jnp.int32, (w_sbs, w_sbs), 0) // CHUNK
                        == lax.broadcasted_iota(jnp.int32, (w_sbs, w_sbs), 1) // CHUNK, 1.0, 0.0).astype(BF16)

    def diag_blocks(res):
        out = res[0:CHUNK]
        for c in range(1, nc):
            out = jnp.where(lblk == c, res[c * CHUNK:(c + 1) * CHUNK], out)
        return out

    def block_diag(y_b):
        return jnp.concatenate([y_b] * nc, axis=0) * bd_mask

    qk_s, kk_s, kt_s = [], [], []
    for hk in range(n_kh):
        ks = slice(hk * HEAD, (hk + 1) * HEAD)
        q_h = q_ref[:, ks]
        k_h = k_ref[:, ks]
        res = _mm_nt(jnp.concatenate([q_h, k_h], axis=0), k_h)
        qk_s.append(diag_blocks(res[:tc]))
        kk_s.append(diag_blocks(res[tc:]))
        kt_s.append(k_h.astype(F32).T)

    t_s, p_s, qkdb_s = [], [], []
    for hv in range(n_vh):
        hk = hv // rep
        g_row = g_t[hv:hv + 1]
        beta_row = beta_t[hv:hv + 1]
        cols = [jnp.broadcast_to(g_c[c * CHUNK:(c + 1) * CHUNK, hv:hv + 1], (CHUNK, LANES)) for c in range(nc)]
        g_col = jnp.concatenate(
            [jnp.where(lane_low, cols[c], cols[c + 1]) for c in range(0, nc, 2)], axis=1)
        decay = jnp.where(tril, jnp.exp(jnp.where(tril, g_col - g_row, 0.0)), 0.0)
        a_t = jnp.where(strict, kk_s[hk] * decay, 0.0) * beta_row
        qkdb_s.append(qk_s[hk] * decay * beta_row)
        t_s.append(jnp.where(eye, 1.0, 0.0) - a_t)
        p_s.append(a_t.astype(BF16))

    for hv in range(n_vh):
        p_s[hv] = _mm(p_s[hv], block_diag(p_s[hv])).astype(BF16)
    power = 2
    while power < CHUNK:
        last = power * 2 >= CHUNK
        for hv in range(n_vh):
            bd = block_diag(p_s[hv])
            t_b = t_s[hv].astype(BF16)
            if last:
                t_s[hv] = t_s[hv] + _mm(t_b, bd)
            else:
                res = _mm(jnp.concatenate([p_s[hv], t_b], axis=0), bd)
                p_s[hv] = res[:CHUNK].astype(BF16)
                t_s[hv] = t_s[hv] + res[CHUNK:]
        power *= 2

    lhs_v, lhs_o, kts = [], [], []
    for hv in range(n_vh):
        eg_row = eg_t[hv:hv + 1]
        tt = t_s[hv]
        te = tt * eg_row
        dg = jnp.where(eye, eg_row, 0.0)
        lhs_v.append([jnp.concatenate([tt[:, c * CHUNK:(c + 1) * CHUNK], -te[:, c * CHUNK:(c + 1) * CHUNK]],
                                      axis=1).astype(BF16) for c in range(nc)])
        lhs_o.append([jnp.concatenate([qkdb_s[hv][:, c * CHUNK:(c + 1) * CHUNK], dg[:, c * CHUNK:(c + 1) * CHUNK]],
                                      axis=1).astype(BF16) for c in range(nc)])
        kts.append((kt_s[hv // rep] * dkb_t[hv:hv + 1]).astype(BF16))

    state = [state_ref[hv] for hv in range(n_vh)]
    for c in range(nc):
        rows = slice(c * CHUNK, (c + 1) * CHUNK)
        ksqs = []
        for hk in range(n_kh):
            ks = slice(hk * HEAD, (hk + 1) * HEAD)
            kq = jnp.concatenate([k_ref[rows, ks], q_ref[rows, ks]], axis=0)
            s_b = jnp.concatenate([state[hk * rep + r].astype(BF16) for r in range(rep)], axis=1)
            ksqs.append(_mm(kq, s_b))
        vts = []
        for hv in range(n_vh):
            hk, r = divmod(hv, rep)
            ks_h = ksqs[hk][:CHUNK, r * HEAD:(r + 1) * HEAD].astype(BF16)
            v_c = v_ref[rows, hv * HEAD:(hv + 1) * HEAD]
            vts.append(_mm(lhs_v[hv][c], jnp.concatenate([v_c, ks_h], axis=0)).astype(BF16))
        for hv in range(n_vh):
            hk, r = divmod(hv, rep)
            qs_h = ksqs[hk][CHUNK:, r * HEAD:(r + 1) * HEAD].astype(BF16)
            out_ref[rows, hv * HEAD:(hv + 1) * HEAD] = _mm(lhs_o[hv][c], jnp.concatenate([vts[hv], qs_h], axis=0))
            state[hv] = state[hv] * egl_rows[c][hv:hv + 1] + _mm(kts[hv][:, rows], vts[hv])
    for hv in range(n_vh):
        state_ref[hv] = state[hv]

    for hv in range(n_vh):
        vs = slice(hv * HEAD, (hv + 1) * HEAD)
        o = out_ref[:, vs]
        o = _rms(o, og_ref[...]) * _silu(z_ref[:, vs].astype(F32))
        out_ref[:, vs] = o
    o_ref[...] = x_ref[...] + _mm(out_ref[...].astype(BF16), wo_ref[...])


def _gdn_delta(x, q, k, v, z, b, a, alog, dtb, out_gain, wo, bsz, seq, n_vh, rep):
    t, d = x.shape
    ns = seq // GDN_TC
    key_dim = q.shape[1]
    val_dim = v.shape[1]
    row = lambda bi, s: (bi * ns + s, 0)
    kern = functools.partial(_gdn_delta_kernel, n_vh=n_vh, rep=rep)
    return pl.pallas_call(
        kern,
        grid=(bsz, ns),
        in_specs=[
            pl.BlockSpec((GDN_TC, d), row),
            pl.BlockSpec((GDN_TC, key_dim), row),
            pl.BlockSpec((GDN_TC, key_dim), row),
            pl.BlockSpec((GDN_TC, val_dim), row),
            pl.BlockSpec((GDN_TC, val_dim), row),
            pl.BlockSpec((GDN_TC, LANES), row),
            pl.BlockSpec((GDN_TC, LANES), row),
            _resident((1, LANES)),
            _resident((1, LANES)),
            _resident((1, HEAD)),
            _resident(wo.shape),
        ],
        out_specs=pl.BlockSpec((GDN_TC, d), row),
        out_shape=jax.ShapeDtypeStruct((t, d), F32),
        scratch_shapes=[
            pltpu.VMEM((n_vh, HEAD, HEAD), F32),
            pltpu.VMEM((GDN_TC, val_dim), F32),
        ],
        compiler_params=_params(("arbitrary", "arbitrary")),
        name="gdn_delta",
    )(x, q, k, v, z, b, a, alog, dtb, out_gain, wo)


def _mla_in_kernel(x_ref, pos_ref, g_ref, win_ref, qg_ref, wq_ref, kvg_ref, wkv_ref, freq_ref, sign_ref,
                   q_ref, k_ref, v_ref, *, q_rank, kv_rank, n_heads, scale):
    tm = x_ref.shape[0]
    h = _rms(x_ref[...], g_ref[...]).astype(BF16)
    proj = _mm(h, win_ref[...])
    c_q = _rms(proj[:, :q_rank], qg_ref[...]).astype(BF16)
    c_kv = _rms(proj[:, q_rank:q_rank + kv_rank], kvg_ref[...]).astype(BF16)
    k_rope = proj[:, q_rank + kv_rank:]

    ang = pos_ref[...].astype(F32) * freq_ref[...]
    lane = lax.broadcasted_iota(jnp.int32, (tm, LANES), 1)
    lower = lane < MLA_ROPE
    table = jnp.where(lower, jnp.cos(ang), jnp.sin(ang) * sign_ref[...])

    def rotate(blk):
        t = blk * table
        return jnp.where(lower, t + pltpu.roll(t, MLA_ROPE, axis=1), 0.0)

    k_rot = rotate(k_rope).astype(BF16)
    q_all = _mm(c_q, wq_ref[...])
    kv_all = _mm(c_kv, wkv_ref[...])
    for hd in range(n_heads):
        qb = q_all[:, hd * MLA_QK:(hd + 1) * MLA_QK]
        q_ref[0, hd, :, 0:HEAD] = (qb[:, :HEAD] * scale).astype(BF16)
        q_ref[0, hd, :, HEAD:MLA_QK] = (rotate(qb[:, HEAD:]) * scale).astype(BF16)
        kvb = kv_all[:, hd * 2 * HEAD:(hd + 1) * 2 * HEAD]
        k_ref[0, hd, :, 0:HEAD] = kvb[:, :HEAD].astype(BF16)
        k_ref[0, hd, :, HEAD:MLA_QK] = k_rot
        v_ref[0, hd, :, :] = kvb[:, HEAD:].astype(BF16)


def _mla_in(x, pos, gain, win, qg, wq, kvg, wkv, freq, sign, bsz, seq, n_heads, q_rank, kv_rank, scale):
    t, d = x.shape
    ns = seq // TM
    row = lambda b, s: (b * ns + s, 0)
    hrow = lambda b, s: (b, 0, s, 0)
    kern = functools.partial(_mla_in_kernel, q_rank=q_rank, kv_rank=kv_rank, n_heads=n_heads, scale=scale)
    return pl.pallas_call(
        kern,
        grid=(bsz, ns),
        in_specs=[
            pl.BlockSpec((TM, d), row),
            pl.BlockSpec((TM, 1), row),
            _resident((1, d)),
            _resident(win.shape),
            _resident((1, q_rank)),
            _resident(wq.shape),
            _resident((1, kv_rank)),
            _resident(wkv.shape),
            _resident((1, LANES)),
            _resident((1, LANES)),
        ],
        out_specs=[
            pl.BlockSpec((1, n_heads, TM, MLA_QK), hrow),
            pl.BlockSpec((1, n_heads, TM, MLA_QK), hrow),
            pl.BlockSpec((1, n_heads, TM, HEAD), hrow),
        ],
        out_shape=[
            jax.ShapeDtypeStruct((bsz, n_heads, seq, MLA_QK), BF16),
            jax.ShapeDtypeStruct((bsz, n_heads, seq, MLA_QK), BF16),
            jax.ShapeDtypeStruct((bsz, n_heads, seq, HEAD), BF16),
        ],
        compiler_params=_params(("parallel", "parallel")),
        name="mla_in",
    )(x, pos, gain, win, qg, wq, kvg, wkv, freq, sign)


def _mla_attn_kernel(q_ref, k_ref, v_ref, o_ref):
    nh = q_ref.shape[1]
    tq = q_ref.shape[2]
    i = pl.program_id(2)
    qs = [q_ref[0, h] for h in range(nh)]

    def step(j, carry, masked):
        kv_rows = pl.ds(pl.multiple_of(j * tq, tq), tq)
        ss = [_mm_nt(qs[h], k_ref[0, h, kv_rows, :]) for h in range(nh)]
        if masked:
            qc = lax.broadcasted_iota(jnp.int32, (tq, tq), 0) // CHUNK
            kc = lax.broadcasted_iota(jnp.int32, (tq, tq), 1) // CHUNK
            ss = [jnp.where(kc <= qc, s, -jnp.inf) for s in ss]
        stats, ps = [], []
        for h in range(nh):
            m, l, _ = carry[h]
            m_new = jnp.maximum(m, jnp.max(ss[h], axis=-1, keepdims=True))
            alpha = jnp.exp(m - m_new)
            p = jnp.exp(ss[h] - m_new)
            stats.append((m_new, alpha * l + jnp.sum(p, axis=-1, keepdims=True), alpha))
            ps.append(p.astype(BF16))
        out = []
        for h in range(nh):
            m_new, l_new, alpha = stats[h]
            out.append((m_new, l_new, alpha * carry[h][2] + _mm(ps[h], v_ref[0, h, kv_rows, :])))
        return tuple(out)

    init = tuple((jnp.full((tq, 1), -jnp.inf, F32), jnp.zeros((tq, 1), F32), jnp.zeros((tq, HEAD), F32))
                 for _ in range(nh))
    carry = lax.fori_loop(0, i, lambda j, c: step(j, c, False), init)
    final = step(i, carry, True)
    for h in range(nh):
        _, l, acc = final[h]
        o_ref[0, :, h * HEAD:(h + 1) * HEAD] = (acc / l).astype(BF16)


def _mla_attn(q, k, v):
    bsz, n_heads, seq, _ = q.shape
    nq = seq // ATT_T
    nh = 2
    return pl.pallas_call(
        _mla_attn_kernel,
        grid=(bsz, n_heads // nh, nq),
        in_specs=[
            pl.BlockSpec((1, nh, ATT_T, MLA_QK), lambda b, h, i: (b, h, i, 0)),
            pl.BlockSpec((1, nh, seq, MLA_QK), lambda b, h, i: (b, h, 0, 0)),
            pl.BlockSpec((1, nh, seq, HEAD), lambda b, h, i: (b, h, 0, 0)),
        ],
        out_specs=pl.BlockSpec((1, ATT_T, nh * HEAD), lambda b, h, i: (b, i, h)),
        out_shape=jax.ShapeDtypeStruct((bsz, seq, n_heads * HEAD), BF16),
        compiler_params=_params(("parallel", "parallel", "arbitrary")),
        name="mla_attn",
    )(q, k, v)


def _proj_residual_kernel(x_ref, a_ref, w_ref, o_ref):
    o_ref[...] = x_ref[...] + _mm(a_ref[...], w_ref[...])


def _proj_residual(x, a, w):
    t, d = x.shape
    return pl.pallas_call(
        _proj_residual_kernel,
        grid=(t // TM,),
        in_specs=[
            pl.BlockSpec((TM, d), lambda i: (i, 0)),
            pl.BlockSpec((TM, a.shape[1]), lambda i: (i, 0)),
            _resident(w.shape),
        ],
        out_specs=pl.BlockSpec((TM, d), lambda i: (i, 0)),
        out_shape=jax.ShapeDtypeStruct((t, d), F32),
        compiler_params=_params(("parallel",)),
        name="mla_out",
    )(x, a, w)


def _pad_row(a):
    return jnp.zeros((1, LANES), F32).at[0, :a.shape[0]].set(a.astype(F32))


def _pad_cols(w):
    return jnp.zeros((w.shape[0], LANES), w.dtype).at[:, :w.shape[1]].set(w)


def _pair_swap(w):
    n = w.shape[-1]
    return w.reshape(*w.shape[:-1], n // 2, 2)[..., ::-1].reshape(w.shape)


def kernel(x, positions, norm_mix, norm_ffn, gdn_w_in, gdn_conv_w, gdn_a_log, gdn_dt_bias, gdn_out_norm, gdn_w_out, mla_w_in, mla_q_norm, mla_w_q_up, mla_kv_norm, mla_w_kv_up, mla_w_out, ffn_w_gate_up, ffn_w_down, final_norm):
    bsz, seq, d = x.shape
    depth = norm_mix.shape[0]
    t = bsz * seq
    d_ff = ffn_w_down.shape[1]
    ff_chunk = 2 * LANES
    n_ff = d_ff // ff_chunk

    n_vh = gdn_a_log.shape[1]
    val_dim = gdn_w_out.shape[1]
    conv_dim = gdn_conv_w.shape[2]
    key_dim = (conv_dim - val_dim) // 2
    rep = n_vh // (key_dim // HEAD)

    q_rank = mla_q_norm.shape[1]
    kv_rank = mla_kv_norm.shape[1]
    n_heads = mla_w_out.shape[1] // HEAD
    scale = (HEAD + MLA_ROPE) ** -0.5

    inv_freq = ROPE_BASE ** (-jnp.arange(0, MLA_ROPE, 2, dtype=F32) / MLA_ROPE)
    freq_row = jnp.tile(jnp.repeat(inv_freq, 2), 2)[None, :]
    sign_row = jnp.tile(jnp.tile(jnp.array([-1.0, 1.0], F32), MLA_ROPE // 2), 2)[None, :]
    pos_col = positions.reshape(t, 1)

    xf = x.reshape(t, d)
    for i in range(depth):
        j = i // 2
        gain = norm_mix[i][None, :]
        if i % 2 == 0:
            w_in = gdn_w_in[j]
            wqkv = w_in[:, :conv_dim].astype(BF16)
            wz = w_in[:, conv_dim:conv_dim + val_dim].astype(BF16)
            w_b = w_in[:, conv_dim + val_dim:conv_dim + val_dim + n_vh]
            w_a = w_in[:, conv_dim + val_dim + n_vh:]
            wba = jnp.concatenate([_pad_cols(w_b), _pad_cols(w_a)], axis=1).astype(BF16)
            q, k, v, z, b, a = _gdn_in(xf, gain, wqkv, wz, wba, gdn_conv_w[j], bsz, seq, key_dim, val_dim)
            xf = _gdn_delta(xf, q, k, v, z, b, a, _pad_row(gdn_a_log[j]), _pad_row(gdn_dt_bias[j]),
                            gdn_out_norm[j][None, :], gdn_w_out[j].astype(BF16), bsz, seq, n_vh, rep)
        else:
            w_in = mla_w_in[j]
            w_kr = w_in[:, q_rank + kv_rank:]
            win = jnp.concatenate([w_in, _pair_swap(w_kr)], axis=1).astype(BF16)
            wq = mla_w_q_up[j].reshape(q_rank, n_heads, HEAD + MLA_ROPE)
            wq = jnp.concatenate([wq, _pair_swap(wq[..., HEAD:])], axis=-1)
            wq = wq.reshape(q_rank, n_heads * MLA_QK).astype(BF16)
            qh, kh, vh = _mla_in(xf, pos_col, gain, win, mla_q_norm[j][None, :], wq,
                                 mla_kv_norm[j][None, :], mla_w_kv_up[j].astype(BF16),
                                 freq_row, sign_row, bsz, seq, n_heads, q_rank, kv_rank, scale)
            att = _mla_attn(qh, kh, vh).reshape(t, n_heads * HEAD)
            xf = _proj_residual(xf, att, mla_w_out[j].astype(BF16))
        wgu = ffn_w_gate_up[i]
        wg = wgu[:, :d_ff].reshape(d, n_ff, ff_chunk).transpose(1, 0, 2).astype(BF16)
        wu = wgu[:, d_ff:].reshape(d, n_ff, ff_chunk).transpose(1, 0, 2).astype(BF16)
        wd = ffn_w_down[i].reshape(n_ff, ff_chunk, d).astype(BF16)
        last = i == depth - 1
        xf = _ffn(xf, norm_ffn[i][None, :], wg, wu, wd, final_norm[None, :], last)
    return xf.reshape(bsz, seq, d)
```

```python
import functools

import jax
import jax.numpy as jnp
from jax import lax
from jax.experimental import pallas as pl
from jax.experimental.pallas import tpu as pltpu

F32 = jnp.float32
BF16 = jnp.bfloat16

NORM_EPS = 1e-6
L2_EPS = 1e-6
CHUNK = 64
ROPE_BASE = 10000.0
LOG2E = 1.4426950408889634
LANES = 128
SUBLANES = 8
VMEM_LIMIT = 56 * 1024 * 1024

HEAD = 128
MLA_ROPE = 64
MLA_QK = 2 * HEAD
V_ROWS = HEAD + 16

TM = 512
FFN_TM = 512
GDN_NC = 4
GDN_TC = GDN_NC * CHUNK
FILL_EVERY = 6
ATT_T = 512
ATT_HEADS = 4
Q_SPLIT = 1
QK_LEAD = 4
FF_CHUNK = 2 * LANES


def _resident(shape):
    nd = len(shape)
    return pl.BlockSpec(shape, lambda *_: (0,) * nd, pipeline_mode=pl.Buffered(1))


def _params(sem, flags=None):
    return pltpu.CompilerParams(dimension_semantics=sem, vmem_limit_bytes=VMEM_LIMIT, flags=flags)


def _rms(x, gain):
    return x * lax.rsqrt(jnp.mean(x * x, axis=-1, keepdims=True) + NORM_EPS) * gain


def _silu(x):
    return x * jax.nn.sigmoid(x)


def _mm(a, b):
    return jnp.dot(a, b, preferred_element_type=F32)


def _mm_nt(a, b):
    return lax.dot_general(a, b, (((1,), (1,)), ((), ())), preferred_element_type=F32)


def _ffn_kernel(*refs, d_ff, final_norm, fused_proj):
    if fused_proj:
        x_ref, a_ref, wo_ref, g_ref, wgu_ref, wd_ref, fg_ref, o_ref = refs
        x = x_ref[...] + _mm(a_ref[...], wo_ref[...])
    else:
        x_ref, g_ref, wgu_ref, wd_ref, fg_ref, o_ref = refs
        x = x_ref[...]
    h = _rms(x, g_ref[...]).astype(BF16)
    acc = x
    for c0 in range(0, d_ff, FF_CHUNK):
        gate = _mm(h, wgu_ref[:, c0:c0 + FF_CHUNK])
        up = _mm(h, wgu_ref[:, d_ff + c0:d_ff + c0 + FF_CHUNK])
        act = (_silu(gate) * up).astype(BF16)
        acc = acc + _mm(act, wd_ref[c0:c0 + FF_CHUNK, :])
    if final_norm:
        acc = _rms(acc, fg_ref[...])
    o_ref[...] = acc


def _ffn(x, gain, wgu, wd, final_gain, final_norm, proj=None):
    t, d = x.shape
    d_ff = wd.shape[0]
    tile = lambda w: pl.BlockSpec((FFN_TM, w), lambda i: (i, 0))
    operands, specs = [x], [tile(d)]
    if proj is not None:
        a, wo = proj
        operands += [a, wo]
        specs += [tile(a.shape[1]), _resident(wo.shape)]
    operands += [gain, wgu, wd, final_gain]
    specs += [_resident((1, d)), _resident(wgu.shape), _resident(wd.shape), _resident((1, d))]
    kern = functools.partial(_ffn_kernel, d_ff=d_ff, final_norm=final_norm, fused_proj=proj is not None)
    return pl.pallas_call(
        kern,
        grid=(t // FFN_TM,),
        in_specs=specs,
        out_specs=tile(d),
        out_shape=jax.ShapeDtypeStruct((t, d), F32),
        compiler_params=_params(("parallel",)),
        name="ffn",
    )(*operands)


def _gdn_project_steps(x_ref, g_ref, wqkv_ref, wz_ref, wba_ref, cw_ref, buf_ref, dst, *, key_dim, conv_k):
    q_dst, k_dst, v_dst, z_dst, b_dst, a_dst = dst
    tm = x_ref.shape[0]
    c_dim = wqkv_ref.shape[1]
    halo = SUBLANES
    q_scale = HEAD ** -0.5
    col_blk = 4 * LANES
    n_blk = c_dim // col_blk
    z_blk = wz_ref.shape[1] // n_blk
    cache = {}

    def hidden():
        if "h" not in cache:
            cache["h"] = _rms(x_ref[...], g_ref[...]).astype(BF16)
        return cache["h"]

    def project(bi):
        h = hidden()
        cols = slice(bi * col_blk, (bi + 1) * col_blk)
        buf_ref[halo:halo + tm, cols] = _mm(h, wqkv_ref[:, cols])
        zs = slice(bi * z_blk, (bi + 1) * z_blk)
        z_dst[:, zs] = _mm(h, wz_ref[:, zs]).astype(BF16)

    def conv(c0):
        cs = slice(c0, c0 + LANES)
        ext = buf_ref[0:halo + tm, cs]
        y = None
        for j in range(conv_k):
            back = conv_k - 1 - j
            frames = pltpu.roll(ext, back, axis=0)[halo:] if back else ext[halo:]
            tap = frames * cw_ref[j:j + 1, cs]
            y = tap if y is None else y + tap
        y = _silu(y)
        if c0 < 2 * key_dim:
            inv = lax.rsqrt(jnp.sum(y * y, axis=-1, keepdims=True) + L2_EPS)
            if c0 < key_dim:
                q_dst[:, cs] = (y * (inv * q_scale)).astype(BF16)
            else:
                k_dst[:, c0 - key_dim:c0 - key_dim + LANES] = (y * inv).astype(BF16)
        else:
            v_dst[:, c0 - 2 * key_dim:c0 - 2 * key_dim + LANES] = y.astype(BF16)
        buf_ref[0:halo, cs] = ext[tm:tm + halo]

    def gates():
        ba = _mm(hidden(), wba_ref[...])
        b_dst[...] = ba[:, :LANES]
        a_dst[...] = ba[:, LANES:]

    steps = []
    for bi in range(n_blk):
        steps.append(functools.partial(project, bi))
        steps += [functools.partial(conv, c0) for c0 in range(bi * col_blk, (bi + 1) * col_blk, LANES)]
    return steps + [gates]


def _gdn_kernel(xp_ref, xd_ref, g_ref, wqkv_ref, wz_ref, wba_ref, cw_ref, alog_ref, dtb_ref, og_ref, wo_ref,
                o_ref, buf_ref, q_s, k_s, v_s, z_s, b_s, a_s, state_ref, out_ref,
                *, key_dim, conv_k, n_vh, rep, tiles_per_seq):
    tc = xd_ref.shape[0]
    nc = tc // CHUNK
    n_kh = n_vh // rep
    w_sbs = nc * CHUNK
    s = pl.program_id(0)
    cur = s % 2
    prev = 1 - cur

    @pl.when(s == 0)
    def _():
        for ref in (buf_ref, q_s, k_s, v_s, z_s, b_s, a_s, state_ref):
            ref[...] = jnp.zeros(ref.shape, ref.dtype)

    @pl.when(s % tiles_per_seq == 0)
    def _():
        buf_ref[0:SUBLANES, :] = jnp.zeros((SUBLANES, buf_ref.shape[1]), F32)

    @pl.when((s + tiles_per_seq - 1) % tiles_per_seq == 0)
    def _():
        state_ref[...] = jnp.zeros(state_ref.shape, F32)

    fillers = _gdn_project_steps(xp_ref, g_ref, wqkv_ref, wz_ref, wba_ref, cw_ref, buf_ref,
                                 (q_s.at[cur], k_s.at[cur], v_s.at[cur], z_s.at[cur], b_s.at[cur], a_s.at[cur]),
                                 key_dim=key_dim, conv_k=conv_k)

    ticks = [0]

    def fill():
        ticks[0] += 1
        if fillers and ticks[0] % FILL_EVERY == 0:
            fillers.pop(0)()

    q_ref, k_ref, v_ref, z_ref = q_s.at[prev], k_s.at[prev], v_s.at[prev], z_s.at[prev]

    beta_c = jax.nn.sigmoid(b_s[prev])
    pre = a_s[prev] + dtb_ref[...]
    softplus = jnp.maximum(pre, 0.0) + jnp.log1p(jnp.exp(-jnp.abs(pre)))
    g_c = -jnp.exp(alog_ref[...]) * softplus
    row_in_chunk = lax.broadcasted_iota(jnp.int32, (tc, LANES), 0) % CHUNK
    shift = 1
    while shift < CHUNK:
        g_c = g_c + jnp.where(row_in_chunk >= shift, pltpu.roll(g_c, shift, axis=0), 0.0)
        shift *= 2
    glast_c = jnp.concatenate(
        [jnp.broadcast_to(g_c[(c + 1) * CHUNK - 1:(c + 1) * CHUNK, :], (CHUNK, LANES)) for c in range(nc)], axis=0)
    g_t = g_c.T[:n_vh]
    beta_t = beta_c.T[:n_vh]
    glast_t = glast_c.T[:n_vh]
    eg_t = jnp.exp(g_t)
    dkb_t = jnp.exp(glast_t - g_t) * beta_t
    egl_t = jnp.exp(glast_t)
    lane_t = lax.broadcasted_iota(jnp.int32, (n_vh, LANES), 1)
    egl_rows = []
    for c in range(nc):
        blk = egl_t[:, (c // 2) * LANES:(c // 2 + 1) * LANES]
        swapped = pltpu.roll(blk, CHUNK, axis=1)
        egl_rows.append(jnp.where((lane_t < CHUNK) == (c % 2 == 0), blk, swapped))

    ri = lax.broadcasted_iota(jnp.int32, (CHUNK, w_sbs), 0)
    li = lax.broadcasted_iota(jnp.int32, (CHUNK, w_sbs), 1)
    lj = li % CHUNK
    lblk = li // CHUNK
    tril = ri >= lj
    strict = ri > lj
    eye = ri == lj
    lane_low = lax.broadcasted_iota(jnp.int32, (CHUNK, LANES), 1) < CHUNK
    bd_mask = jnp.where(lax.broadcasted_iota(jnp.int32, (w_sbs, w_sbs), 0) // CHUNK
                        == lax.broadcasted_iota(jnp.int32, (w_sbs, w_sbs), 1) // CHUNK, 1.0, 0.0).astype(BF16)

    def diag_blocks(res):
        out = res[0:CHUNK]
        for c in range(1, nc):
            out = jnp.where(lblk == c, res[c * CHUNK:(c + 1) * CHUNK], out)
        return out

    def block_diag(y_b):
        return jnp.concatenate([y_b] * nc, axis=0) * bd_mask

    qk_s, kk_s, kt_s = [], [], []
    for hk in range(n_kh):
        ks = slice(hk * HEAD, (hk + 1) * HEAD)
        q_h = q_ref[:, ks]
        k_h = k_ref[:, ks]
        res = _mm_nt(jnp.concatenate([q_h, k_h], axis=0), k_h)
        qk_s.append(diag_blocks(res[:tc]))
        kk_s.append(diag_blocks(res[tc:]))
        kt_s.append(k_h.astype(F32).T)
        fill()

    t_s, p_s, qkdb_s = [], [], []
    for hv in range(n_vh):
        hk = hv // rep
        g_row = g_t[hv:hv + 1]
        beta_row = beta_t[hv:hv + 1]
        cols = [jnp.broadcast_to(g_c[c * CHUNK:(c + 1) * CHUNK, hv:hv + 1], (CHUNK, LANES)) for c in range(nc)]
        g_col = jnp.concatenate(
            [jnp.where(lane_low, cols[c], cols[c + 1]) for c in range(0, nc, 2)], axis=1)
        decay = jnp.where(tril, jnp.exp(jnp.where(tril, g_col - g_row, 0.0)), 0.0)
        a_t = jnp.where(strict, kk_s[hk] * decay, 0.0) * beta_row
        qkdb_s.append(qk_s[hk] * decay * beta_row)
        t_s.append(jnp.where(eye, 1.0, 0.0) - a_t)
        p_s.append(a_t.astype(BF16))
        fill()

    for hv in range(n_vh):
        p_s[hv] = _mm(p_s[hv], block_diag(p_s[hv])).astype(BF16)
        fill()
    power = 2
    while power < CHUNK:
        last = power * 2 >= CHUNK
        for hv in range(n_vh):
            bd = block_diag(p_s[hv])
            t_b = t_s[hv].astype(BF16)
            if last:
                t_s[hv] = t_s[hv] + _mm(t_b, bd)
            else:
                res = _mm(jnp.concatenate([p_s[hv], t_b], axis=0), bd)
                p_s[hv] = res[:CHUNK].astype(BF16)
                t_s[hv] = t_s[hv] + res[CHUNK:]
            fill()
        power *= 2

    lhs_v, lhs_o, kts = [], [], []
    for hv in range(n_vh):
        eg_row = eg_t[hv:hv + 1]
        tt = t_s[hv]
        te = tt * eg_row
        dg = jnp.where(eye, eg_row, 0.0)
        lhs_v.append([jnp.concatenate([tt[:, c * CHUNK:(c + 1) * CHUNK], -te[:, c * CHUNK:(c + 1) * CHUNK]],
                                      axis=1).astype(BF16) for c in range(nc)])
        lhs_o.append([jnp.concatenate([qkdb_s[hv][:, c * CHUNK:(c + 1) * CHUNK], dg[:, c * CHUNK:(c + 1) * CHUNK]],
                                      axis=1).astype(BF16) for c in range(nc)])
        kts.append((kt_s[hv // rep] * dkb_t[hv:hv + 1]).astype(BF16))

    state = [state_ref[hv] for hv in range(n_vh)]
    for c in range(nc):
        rows = slice(c * CHUNK, (c + 1) * CHUNK)
        ksqs = []
        for hk in range(n_kh):
            ks = slice(hk * HEAD, (hk + 1) * HEAD)
            kq = jnp.concatenate([k_ref[rows, ks], q_ref[rows, ks]], axis=0)
            s_b = jnp.concatenate([state[hk * rep + r].astype(BF16) for r in range(rep)], axis=1)
            ksqs.append(_mm(kq, s_b))
            fill()
        vts = []
        for hv in range(n_vh):
            hk, r = divmod(hv, rep)
            ks_h = ksqs[hk][:CHUNK, r * HEAD:(r + 1) * HEAD].astype(BF16)
            v_c = v_ref[rows, hv * HEAD:(hv + 1) * HEAD]
            vts.append(_mm(lhs_v[hv][c], jnp.concatenate([v_c, ks_h], axis=0)).astype(BF16))
            fill()
        for hv in range(n_vh):
            hk, r = divmod(hv, rep)
            qs_h = ksqs[hk][CHUNK:, r * HEAD:(r + 1) * HEAD].astype(BF16)
            out_ref[rows, hv * HEAD:(hv + 1) * HEAD] = _mm(lhs_o[hv][c], jnp.concatenate([vts[hv], qs_h], axis=0))
            state[hv] = state[hv] * egl_rows[c][hv:hv + 1] + _mm(kts[hv][:, rows], vts[hv])
            fill()
    for hv in range(n_vh):
        state_ref[hv] = state[hv]
    while fillers:
        fillers.pop(0)()

    for hv in range(n_vh):
        vs = slice(hv * HEAD, (hv + 1) * HEAD)
        o = out_ref[:, vs]
        o = _rms(o, og_ref[...]) * _silu(z_ref[:, vs].astype(F32))
        out_ref[:, vs] = o
    o_ref[...] = xd_ref[...] + _mm(out_ref[...].astype(BF16), wo_ref[...])


def _gdn_layer(x, gain, wqkv, wz, wba, conv_w, alog, dtb, out_gain, wo, seq, key_dim, val_dim, n_vh, rep):
    t, d = x.shape
    n_tiles = t // GDN_TC
    c_dim = wqkv.shape[1]
    proj_row = lambda s: (jnp.minimum(s, n_tiles - 1), 0)
    delta_row = lambda s: (jnp.maximum(s - 1, 0), 0)
    kern = functools.partial(_gdn_kernel, key_dim=key_dim, conv_k=conv_w.shape[0], n_vh=n_vh, rep=rep,
                             tiles_per_seq=seq // GDN_TC)
    return pl.pallas_call(
        kern,
        grid=(n_tiles + 1,),
        in_specs=[
            pl.BlockSpec((GDN_TC, d), proj_row),
            pl.BlockSpec((GDN_TC, d), delta_row),
            _resident((1, d)),
            _resident(wqkv.shape),
            _resident(wz.shape),
            _resident(wba.shape),
            _resident(conv_w.shape),
            _resident((1, LANES)),
            _resident((1, LANES)),
            _resident((1, HEAD)),
            _resident(wo.shape),
        ],
        out_specs=pl.BlockSpec((GDN_TC, d), delta_row),
        out_shape=jax.ShapeDtypeStruct((t, d), F32),
        scratch_shapes=[
            pltpu.VMEM((GDN_TC + SUBLANES, c_dim), F32),
            pltpu.VMEM((2, GDN_TC, key_dim), BF16),
            pltpu.VMEM((2, GDN_TC, key_dim), BF16),
            pltpu.VMEM((2, GDN_TC, val_dim), BF16),
            pltpu.VMEM((2, GDN_TC, val_dim), BF16),
            pltpu.VMEM((2, GDN_TC, LANES), F32),
            pltpu.VMEM((2, GDN_TC, LANES), F32),
            pltpu.VMEM((n_vh, HEAD, HEAD), F32),
            pltpu.VMEM((GDN_TC, val_dim), F32),
        ],
        compiler_params=_params(("arbitrary",)),
        name="gdn",
    )(x, x, gain, wqkv, wz, wba, conv_w, alog, dtb, out_gain, wo)


def _mla_in_kernel(x_ref, pos_ref, g_ref, win_ref, qg_ref, wq_ref, kvg_ref, wkv_ref, freq_ref, sign_ref,
                   q_ref, k_ref, v_ref, *, q_rank, kv_rank, n_heads, scale):
    tm = x_ref.shape[0]
    h = _rms(x_ref[...], g_ref[...]).astype(BF16)
    proj = _mm(h, win_ref[...])
    c_q = _rms(proj[:, :q_rank], qg_ref[...]).astype(BF16)
    c_kv = _rms(proj[:, q_rank:q_rank + kv_rank], kvg_ref[...]).astype(BF16)
    k_rope = proj[:, q_rank + kv_rank:]

    ang = pos_ref[...].astype(F32) * freq_ref[...]
    lane = lax.broadcasted_iota(jnp.int32, (tm, LANES), 1)
    lower = lane < MLA_ROPE
    table = jnp.where(lower, jnp.cos(ang), jnp.sin(ang) * sign_ref[...])

    def rotate(blk):
        t = blk * table
        return jnp.where(lower, t + pltpu.roll(t, MLA_ROPE, axis=1), 0.0)

    k_rot = rotate(k_rope).astype(BF16)
    q_all = _mm(c_q, wq_ref[...])
    kv_all = _mm(c_kv, wkv_ref[...])
    ones_rows = jnp.where(lax.broadcasted_iota(jnp.int32, (V_ROWS - HEAD, ATT_T), 0) == 0, 1.0, 0.0).astype(BF16)
    for hd in range(n_heads):
        qb = q_all[:, hd * MLA_QK:(hd + 1) * MLA_QK]
        q_ref[0, hd, :, 0:HEAD] = (qb[:, :HEAD] * scale).astype(BF16)
        q_ref[0, hd, :, HEAD:MLA_QK] = (rotate(qb[:, HEAD:]) * scale).astype(BF16)
        kvb = kv_all[:, hd * 2 * HEAD:(hd + 1) * 2 * HEAD]
        k_ref[0, hd, :, 0:HEAD] = kvb[:, :HEAD].astype(BF16)
        k_ref[0, hd, :, HEAD:MLA_QK] = k_rot
        for jj in range(tm // ATT_T):
            v_ref[0, hd, jj, 0:HEAD, :] = kvb[jj * ATT_T:(jj + 1) * ATT_T, HEAD:].T.astype(BF16)
            v_ref[0, hd, jj, HEAD:V_ROWS, :] = ones_rows


def _mla_in(x, pos, gain, win, qg, wq, kvg, wkv, freq, sign, bsz, seq, n_heads, q_rank, kv_rank, scale):
    t, d = x.shape
    ns = seq // TM
    row = lambda b, s: (b * ns + s, 0)
    hrow = lambda b, s: (b, 0, s, 0)
    kern = functools.partial(_mla_in_kernel, q_rank=q_rank, kv_rank=kv_rank, n_heads=n_heads, scale=scale)
    return pl.pallas_call(
        kern,
        grid=(bsz, ns),
        in_specs=[
            pl.BlockSpec((TM, d), row),
            pl.BlockSpec((TM, 1), row),
            _resident((1, d)),
            _resident(win.shape),
            _resident((1, q_rank)),
            _resident(wq.shape),
            _resident((1, kv_rank)),
            _resident(wkv.shape),
            _resident((1, LANES)),
            _resident((1, LANES)),
        ],
        out_specs=[
            pl.BlockSpec((1, n_heads, TM, MLA_QK), hrow),
            pl.BlockSpec((1, n_heads, TM, MLA_QK), hrow),
            pl.BlockSpec((1, n_heads, TM // ATT_T, V_ROWS, ATT_T), lambda b, s: (b, 0, s, 0, 0)),
        ],
        out_shape=[
            jax.ShapeDtypeStruct((bsz, n_heads, seq, MLA_QK), BF16),
            jax.ShapeDtypeStruct((bsz, n_heads, seq, MLA_QK), BF16),
            jax.ShapeDtypeStruct((bsz, n_heads, seq // ATT_T, V_ROWS, ATT_T), BF16),
        ],
        compiler_params=_params(("parallel", "parallel")),
        name="mla_in",
    )(x, pos, gain, win, qg, wq, kvg, wkv, freq, sign)


def _mla_attn_kernel(q_ref, k_ref, vt_ref, o_ref):
    nh = q_ref.shape[1]
    tq = q_ref.shape[2]
    tp = tq // Q_SPLIT
    i = pl.program_id(2)
    streams = [(h, r) for h in range(nh) for r in range(Q_SPLIT)]
    qs = {(h, r): q_ref[0, h, r * tp:(r + 1) * tp, :] for h, r in streams}

    def step(j, carry, masked):
        kv_rows = pl.ds(pl.multiple_of(j * tq, tq), tq)
        if masked:
            kc = lax.broadcasted_iota(jnp.int32, (tq, tq), 0) // CHUNK
            qc = lax.broadcasted_iota(jnp.int32, (tq, tq), 1) // CHUNK
            visible = kc <= qc

        def scores(st):
            h, r = st
            s = _mm_nt(k_ref[0, h, kv_rows, :], qs[st])
            return jnp.where(visible[:, r * tp:(r + 1) * tp], s, -jnp.inf) if masked else s

        ss = {n: scores(streams[n]) for n in range(min(QK_LEAD, len(streams)))}
        out = []
        for n, (h, r) in enumerate(streams):
            m, acc = carry[n]
            s = ss.pop(n)
            m_new = jnp.maximum(m, jnp.max(s, axis=0, keepdims=True))
            alpha = jnp.exp2(m - m_new)
            p = jnp.exp2(s - m_new)
            out.append((m_new, alpha * acc + _mm(vt_ref[0, h, j], p.astype(BF16))))
            if n + QK_LEAD < len(streams):
                ss[n + QK_LEAD] = scores(streams[n + QK_LEAD])
        return tuple(out)

    init = tuple((jnp.full((1, tp), -jnp.inf, F32), jnp.zeros((V_ROWS, tp), F32)) for _ in streams)
    carry = lax.fori_loop(0, i, lambda j, c: step(j, c, False), init)
    final = step(i, carry, True)
    for n, (h, r) in enumerate(streams):
        _, acc = final[n]
        o_ref[0, r * tp:(r + 1) * tp, h * HEAD:(h + 1) * HEAD] = (acc[:HEAD] / acc[HEAD:HEAD + 1]).T.astype(BF16)


def _mla_attn(q, k, v):
    bsz, n_heads, seq, _ = q.shape
    nq = seq // ATT_T
    nh = ATT_HEADS
    return pl.pallas_call(
        _mla_attn_kernel,
        grid=(bsz, n_heads // nh, nq),
        in_specs=[
            pl.BlockSpec((1, nh, ATT_T, MLA_QK), lambda b, h, i: (b, h, i, 0)),
            pl.BlockSpec((1, nh, seq, MLA_QK), lambda b, h, i: (b, h, 0, 0)),
            pl.BlockSpec((1, nh, nq, V_ROWS, ATT_T), lambda b, h, i: (b, h, 0, 0, 0)),
        ],
        out_specs=pl.BlockSpec((1, ATT_T, nh * HEAD), lambda b, h, i: (b, i, h)),
        out_shape=jax.ShapeDtypeStruct((bsz, seq, n_heads * HEAD), BF16),
        compiler_params=_params(("parallel", "parallel", "arbitrary")),
        name="mla_attn",
    )(q, k, v)


def _pad_row(a):
    return jnp.zeros((1, LANES), F32).at[0, :a.shape[0]].set(a.astype(F32))


def _pad_cols(w):
    return jnp.zeros((w.shape[0], LANES), w.dtype).at[:, :w.shape[1]].set(w)


def _pair_swap(w):
    n = w.shape[-1]
    return w.reshape(*w.shape[:-1], n // 2, 2)[..., ::-1].reshape(w.shape)


def kernel(x, positions, norm_mix, norm_ffn, gdn_w_in, gdn_conv_w, gdn_a_log, gdn_dt_bias, gdn_out_norm, gdn_w_out, mla_w_in, mla_q_norm, mla_w_q_up, mla_kv_norm, mla_w_kv_up, mla_w_out, ffn_w_gate_up, ffn_w_down, final_norm):
    bsz, seq, d = x.shape
    depth = norm_mix.shape[0]
    t = bsz * seq

    n_vh = gdn_a_log.shape[1]
    val_dim = gdn_w_out.shape[1]
    conv_dim = gdn_conv_w.shape[2]
    key_dim = (conv_dim - val_dim) // 2
    rep = n_vh // (key_dim // HEAD)

    q_rank = mla_q_norm.shape[1]
    kv_rank = mla_kv_norm.shape[1]
    n_heads = mla_w_out.shape[1] // HEAD
    scale = (HEAD + MLA_ROPE) ** -0.5 * LOG2E

    inv_freq = ROPE_BASE ** (-jnp.arange(0, MLA_ROPE, 2, dtype=F32) / MLA_ROPE)
    freq_row = jnp.tile(jnp.repeat(inv_freq, 2), 2)[None, :]
    sign_row = jnp.tile(jnp.tile(jnp.array([-1.0, 1.0], F32), MLA_ROPE // 2), 2)[None, :]
    pos_col = positions.reshape(t, 1)

    xf = x.reshape(t, d)
    for i in range(depth):
        j = i // 2
        gain = norm_mix[i][None, :]
        proj = None
        if i % 2 == 0:
            w_in = gdn_w_in[j]
            wqkv = w_in[:, :conv_dim].astype(BF16)
            wz = w_in[:, conv_dim:conv_dim + val_dim].astype(BF16)
            w_b = w_in[:, conv_dim + val_dim:conv_dim + val_dim + n_vh]
            w_a = w_in[:, conv_dim + val_dim + n_vh:]
            wba = jnp.concatenate([_pad_cols(w_b), _pad_cols(w_a)], axis=1).astype(BF16)
            xf = _gdn_layer(xf, gain, wqkv, wz, wba, gdn_conv_w[j], _pad_row(gdn_a_log[j]), _pad_row(gdn_dt_bias[j]),
                            gdn_out_norm[j][None, :], gdn_w_out[j].astype(BF16), seq, key_dim, val_dim, n_vh, rep)
        else:
            w_in = mla_w_in[j]
            w_kr = w_in[:, q_rank + kv_rank:]
            win = jnp.concatenate([w_in, _pair_swap(w_kr)], axis=1).astype(BF16)
            wq = mla_w_q_up[j].reshape(q_rank, n_heads, HEAD + MLA_ROPE)
            wq = jnp.concatenate([wq, _pair_swap(wq[..., HEAD:])], axis=-1)
            wq = wq.reshape(q_rank, n_heads * MLA_QK).astype(BF16)
            qh, kh, vh = _mla_in(xf, pos_col, gain, win, mla_q_norm[j][None, :], wq,
                                 mla_kv_norm[j][None, :], mla_w_kv_up[j].astype(BF16),
                                 freq_row, sign_row, bsz, seq, n_heads, q_rank, kv_rank, scale)
            att = _mla_attn(qh, kh, vh).reshape(t, n_heads * HEAD)
            proj = (att, mla_w_out[j].astype(BF16))
        last = i == depth - 1
        xf = _ffn(xf, norm_ffn[i][None, :], ffn_w_gate_up[i].astype(BF16), ffn_w_down[i].astype(BF16),
                  final_norm[None, :], last, proj)
    return xf.reshape(bsz, seq, d)
```
